```python
import math
import jax, jax.numpy as jnp
from jax import lax
import numpy as np

D_MODEL = 2048
BATCH = 1
SEQ = 8192
DEPTH = 1
DEC_BATCH = 128
DEC_SEQ = 1
PAST_LEN = 2048
PAGE_SIZE = 128

SSM_WIDTH = D_MODEL // 2
SSM_GROUP_SIZE = 16
SSM_GROUPS = SSM_WIDTH // SSM_GROUP_SIZE
SSM_STATE = 64
DT_MIN = 1e-3
DT_MAX = 1e-1
SB_HEAD_DIM = 128
SB_WIDTH = D_MODEL // 2
SB_HEADS = SB_WIDTH // SB_HEAD_DIM
SB_SCALE = 1.0 / math.sqrt(SB_HEAD_DIM)
SB_BIAS_LO = -7.0
SB_BIAS_HI = -5.0
Q_BLOCK = 128
RMS_EPS = 1e-6
SPLITS = (SSM_WIDTH, 2 * SSM_WIDTH, 2 * SSM_WIDTH + SB_WIDTH, 2 * SSM_WIDTH + 2 * SB_WIDTH,
          2 * SSM_WIDTH + 3 * SB_WIDTH, 2 * SSM_WIDTH + 4 * SB_WIDTH,
          2 * SSM_WIDTH + 4 * SB_WIDTH + D_MODEL)
N_IN = 2 * SSM_WIDTH + 4 * SB_WIDTH + 2 * D_MODEL

kernel_name = 'hybrid_s5_stickbreaking_gated_step'


def rmsnorm(x, g):
    xf = x.astype(jnp.float32)
    y = xf * lax.rsqrt(jnp.mean(xf * xf, axis=-1, keepdims=True) + RMS_EPS)
    return (y * g.astype(jnp.float32)).astype(x.dtype)


def s5_scan(u, h0, A_re, A_im, log_dt, B_re, B_im, C_re, C_im, D):
    f32 = jnp.float32
    b, L, _ = u.shape
    uf = u.astype(f32).reshape(b, L, SSM_GROUPS, SSM_GROUP_SIZE)
    lam = lax.complex(A_re.astype(f32), A_im.astype(f32))
    dt = jnp.exp(log_dt.astype(f32))[:, None]
    a_bar = jnp.exp(lam * dt)
    b_bar = ((a_bar - 1.0) / lam)[:, :, None] * lax.complex(B_re.astype(f32), B_im.astype(f32))
    bu = jnp.einsum('blgc,gpc->blgp', uf.astype(jnp.complex64), b_bar)
    bu = bu.at[:, 0].add(a_bar[None] * h0)
    a_seq = jnp.broadcast_to(a_bar, bu.shape)

    def combine(left, right):
        a_l, b_l = left
        a_r, b_r = right
        return a_l * a_r, a_r * b_l + b_r

    _, h = lax.associative_scan(combine, (a_seq, bu), axis=1)
    c = lax.complex(C_re.astype(f32), C_im.astype(f32))
    y = jnp.real(jnp.einsum('blgp,gcp->blgc', h, c)) + D.astype(f32).reshape(SSM_GROUPS, SSM_GROUP_SIZE) * uf
    return y.reshape(b, L, SSM_WIDTH), h[:, -1]


def sb_block(q_blk, pos_blk, k, v, k_pos, bias):
    z = jnp.einsum('bqhd,bkhd->bhqk', q_blk.astype(jnp.float32), k) * SB_SCALE + bias[None, :, None, None]
    valid = k_pos[None, :] < pos_blk[:, None]
    log_keep = jnp.where(valid, -jax.nn.softplus(z), 0.0)
    later = lax.cumsum(log_keep, axis=3, reverse=True) - log_keep
    w = jnp.where(valid, jnp.exp(jax.nn.log_sigmoid(z) + later), 0.0)
    return jnp.einsum('bhqk,bkhd->bqhd', w, v)


def sb_sweep(q, k, v, q_pos, k_pos, bias):
    b, lq, h, d = q.shape
    qb = min(Q_BLOCK, lq)
    nb = -(-lq // qb)
    pad = nb * qb - lq
    qp = jnp.pad(q, ((0, 0), (0, pad), (0, 0), (0, 0)))
    pp = jnp.pad(q_pos, (0, pad))
    q_blocks = qp.reshape(b, nb, qb, h, d).transpose(1, 0, 2, 3, 4)
    pos_blocks = pp.reshape(nb, qb)
    kf = k.astype(jnp.float32)
    vf = v.astype(jnp.float32)
    bf = bias.astype(jnp.float32)
    o = lax.map(lambda blk: sb_block(blk[0], blk[1], kf, vf, k_pos, bf), (q_blocks, pos_blocks))
    o = o.transpose(1, 0, 2, 3, 4).reshape(b, nb * qb, h, d)[:, :lq]
    return o.astype(q.dtype)


def mixer_layer(x, h0, past_k, past_v, q_pos, k_pos, p):
    (norm_g, w_in, A_re, A_im, log_dt, B_re, B_im, C_re, C_im, D,
     glu_w, glu_b, sb_bias, w_branch_a, w_branch_b, w_out) = p
    b, L, _ = x.shape
    xn = rmsnorm(x, norm_g)
    proj = xn @ w_in
    u_a, z_a, q, k, v, z_b, g_a, g_b = jnp.split(proj, SPLITS, axis=-1)
    y_a, h_last = s5_scan(u_a, h0, A_re, A_im, log_dt, B_re, B_im, C_re, C_im, D)
    y_a = jax.nn.gelu(y_a, approximate=False)
    y_a = y_a * jax.nn.sigmoid(y_a @ glu_w.astype(jnp.float32) + glu_b.astype(jnp.float32))
    y_a = y_a.astype(x.dtype) * jax.nn.silu(z_a)
    q = q.reshape(b, L, SB_HEADS, SB_HEAD_DIM)
    k = k.reshape(b, L, SB_HEADS, SB_HEAD_DIM)
    v = v.reshape(b, L, SB_HEADS, SB_HEAD_DIM)
    k_all = jnp.concatenate([past_k.astype(k.dtype), k], axis=1)
    v_all = jnp.concatenate([past_v.astype(v.dtype), v], axis=1)
    o = sb_sweep(q, k_all, v_all, q_pos, k_pos, sb_bias).reshape(b, L, SB_WIDTH) * jax.nn.silu(z_b)
    merged = jax.nn.sigmoid(g_a) * (y_a @ w_branch_a) + jax.nn.sigmoid(g_b) * (o @ w_branch_b)
    return x + merged @ w_out, k, v, h_last


def setup_inputs(seed: int = 0) -> dict:
    key = jax.random.key(seed)
    ks = jax.random.split(key, 24)
    f32 = jnp.float32
    n_pages = PAST_LEN // PAGE_SIZE
    n_phys = (DEC_BATCH * n_pages * 5) // 4

    def nrm(k, shape, scale):
        return jax.random.normal(k, shape, f32) * scale

    x_prompt = nrm(ks[0], (BATCH, SEQ, D_MODEL), 1.0)
    x_sample = nrm(ks[1], (DEC_BATCH, DEC_SEQ, D_MODEL), 1.0)
    cache_k = nrm(ks[2], (DEPTH, n_phys, PAGE_SIZE, SB_HEADS, SB_HEAD_DIM), 1.0)
    cache_v = nrm(ks[3], (DEPTH, n_phys, PAGE_SIZE, SB_HEADS, SB_HEAD_DIM), 1.0)
    state_ssm_re = nrm(ks[4], (DEPTH, DEC_BATCH, SSM_GROUPS, SSM_STATE), 0.1)
    state_ssm_im = nrm(ks[5], (DEPTH, DEC_BATCH, SSM_GROUPS, SSM_STATE), 0.1)
    page_table = jax.random.permutation(ks[6], n_phys)[: DEC_BATCH * n_pages].reshape(DEC_BATCH, n_pages).astype(jnp.int32)
    norm_g = 1.0 + nrm(ks[7], (DEPTH, D_MODEL), 0.02)
    w_in = nrm(ks[8], (DEPTH, D_MODEL, N_IN), D_MODEL ** -0.5)
    n = jnp.arange(SSM_STATE, dtype=f32)
    ssm_A_re = -0.5 + nrm(ks[9], (DEPTH, SSM_GROUPS, SSM_STATE), 0.01)
    ssm_A_im = math.pi * n + nrm(ks[10], (DEPTH, SSM_GROUPS, SSM_STATE), 0.01)
    ssm_log_dt = jax.random.uniform(ks[11], (DEPTH, SSM_GROUPS), f32, math.log(DT_MIN), math.log(DT_MAX))
    ssm_B_re = nrm(ks[12], (DEPTH, SSM_GROUPS, SSM_STATE, SSM_GROUP_SIZE), (2 * SSM_GROUP_SIZE) ** -0.5)
    ssm_B_im = nrm(ks[13], (DEPTH, SSM_GROUPS, SSM_STATE, SSM_GROUP_SIZE), (2 * SSM_GROUP_SIZE) ** -0.5)
    ssm_C_re = nrm(ks[14], (DEPTH, SSM_GROUPS, SSM_GROUP_SIZE, SSM_STATE), (2 * SSM_STATE) ** -0.5)
    ssm_C_im = nrm(ks[15], (DEPTH, SSM_GROUPS, SSM_GROUP_SIZE, SSM_STATE), (2 * SSM_STATE) ** -0.5)
    ssm_D = nrm(ks[16], (DEPTH, SSM_WIDTH), 1.0)
    glu_w = nrm(ks[17], (DEPTH, SSM_WIDTH, SSM_WIDTH), SSM_WIDTH ** -0.5)
    glu_b = nrm(ks[18], (DEPTH, SSM_WIDTH), 0.01)
    sb_bias = jax.random.uniform(ks[23], (DEPTH, SB_HEADS), f32, SB_BIAS_LO, SB_BIAS_HI)
    w_branch_a = nrm(ks[19], (DEPTH, SSM_WIDTH, D_MODEL), SSM_WIDTH ** -0.5)
    w_branch_b = nrm(ks[20], (DEPTH, SB_WIDTH, D_MODEL), SB_WIDTH ** -0.5)
    w_out = nrm(ks[21], (DEPTH, D_MODEL, D_MODEL), D_MODEL ** -0.5)
    norm_f = 1.0 + nrm(ks[22], (D_MODEL,), 0.02)
    return {'x_prompt': x_prompt, 'x_sample': x_sample, 'cache_k': cache_k, 'cache_v': cache_v,
            'state_ssm_re': state_ssm_re, 'state_ssm_im': state_ssm_im, 'page_table': page_table,
            'norm_g': norm_g, 'w_in': w_in, 'ssm_A_re': ssm_A_re, 'ssm_A_im': ssm_A_im,
            'ssm_log_dt': ssm_log_dt, 'ssm_B_re': ssm_B_re, 'ssm_B_im': ssm_B_im,
            'ssm_C_re': ssm_C_re, 'ssm_C_im': ssm_C_im, 'ssm_D': ssm_D, 'glu_w': glu_w, 'glu_b': glu_b,
            'sb_bias': sb_bias, 'w_branch_a': w_branch_a, 'w_branch_b': w_branch_b, 'w_out': w_out,
            'norm_f': norm_f}


def reference(x_prompt, x_sample, cache_k, cache_v, state_ssm_re, state_ssm_im, page_table,
              norm_g, w_in, ssm_A_re, ssm_A_im, ssm_log_dt, ssm_B_re, ssm_B_im, ssm_C_re, ssm_C_im,
              ssm_D, glu_w, glu_b, sb_bias, w_branch_a, w_branch_b, w_out, norm_f):
    f32 = jnp.float32
    bp, seq = x_prompt.shape[0], x_prompt.shape[1]
    dec_b, dec_seq = x_sample.shape[0], x_sample.shape[1]
    n_pages = page_table.shape[1]
    past_len = n_pages * cache_k.shape[2]
    q_pos_p = jnp.arange(seq, dtype=jnp.int32)
    k_pos_p = q_pos_p
    q_pos_s = past_len + jnp.arange(dec_seq, dtype=jnp.int32)
    k_pos_s = jnp.arange(past_len + dec_seq, dtype=jnp.int32)
    hp, hs = x_prompt, x_sample
    kp_l, vp_l, ks_l, vs_l = [], [], [], []
    srp_l, sip_l, srs_l, sis_l = [], [], [], []
    for l in range(DEPTH):
        p = (norm_g[l], w_in[l], ssm_A_re[l], ssm_A_im[l], ssm_log_dt[l], ssm_B_re[l], ssm_B_im[l],
             ssm_C_re[l], ssm_C_im[l], ssm_D[l], glu_w[l], glu_b[l], sb_bias[l],
             w_branch_a[l], w_branch_b[l], w_out[l])
        h0_p = jnp.zeros((bp, SSM_GROUPS, SSM_STATE), jnp.complex64)
        empty = jnp.zeros((bp, 0, SB_HEADS, SB_HEAD_DIM), x_prompt.dtype)
        hp, kp, vp, sp = mixer_layer(hp, h0_p, empty, empty, q_pos_p, k_pos_p, p)
        past_k = cache_k[l][page_table].reshape(dec_b, past_len, SB_HEADS, SB_HEAD_DIM)
        past_v = cache_v[l][page_table].reshape(dec_b, past_len, SB_HEADS, SB_HEAD_DIM)
        h0_s = lax.complex(state_ssm_re[l].astype(f32), state_ssm_im[l].astype(f32))
        hs, ks_, vs_, ss = mixer_layer(hs, h0_s, past_k, past_v, q_pos_s, k_pos_s, p)
        kp_l.append(kp)
        vp_l.append(vp)
        ks_l.append(ks_)
        vs_l.append(vs_)
        srp_l.append(jnp.real(sp))
        sip_l.append(jnp.imag(sp))
        srs_l.append(jnp.real(ss))
        sis_l.append(jnp.imag(ss))
    y_prompt = rmsnorm(hp, norm_f)
    y_sample = rmsnorm(hs, norm_f)
    k_prompt = jnp.stack(kp_l, axis=0)
    v_prompt = jnp.stack(vp_l, axis=0)
    k_sample = jnp.stack(ks_l, axis=0)
    v_sample = jnp.stack(vs_l, axis=0)
    ssm_re_prompt = jnp.stack(srp_l, axis=0)
    ssm_im_prompt = jnp.stack(sip_l, axis=0)
    ssm_re_sample = jnp.stack(srs_l, axis=0)
    ssm_im_sample = jnp.stack(sis_l, axis=0)
    return (y_prompt, y_sample, k_prompt, v_prompt, k_sample, v_sample,
            ssm_re_prompt, ssm_im_prompt, ssm_re_sample, ssm_im_sample)
```

```python
import math

import jax
import jax.numpy as jnp
from jax import lax
from jax.experimental import pallas as pl
from jax.experimental.pallas import tpu as pltpu

F32 = jnp.float32
BF16 = jnp.bfloat16

D_MODEL = 2048
SSM_WIDTH = 1024
SSM_GROUP_SIZE = 16
SSM_GROUPS = 64
SSM_STATE = 64
NS = SSM_GROUPS * SSM_STATE
SB_HEADS = 8
SB_HEAD_DIM = 128
SB_WIDTH = 1024
SB_SCALE = 1.0 / math.sqrt(SB_HEAD_DIM)
RMS_EPS = 1e-6
N_IN = 10240
PAGE = 128

U_BLK_1024, ZA_BLK_1024, ZB_BLK_1024 = 0, 1, 5
GA_BLK_2048, GB_BLK_2048 = 3, 4
Q_BLK_128, K_BLK_128, V_BLK_128 = 16, 24, 32

TQ = 256
SCAN_T = 128
VMEM_LIMIT_BYTES = 56 * 1024 * 1024


def _params(n_axes):
    return pltpu.CompilerParams(dimension_semantics=("arbitrary",) * n_axes,
                                vmem_limit_bytes=VMEM_LIMIT_BYTES)


def _sigmoid(x):
    return jax.nn.sigmoid(x)


def _rms_kernel(x_ref, g_ref, o_ref):
    x = x_ref[...]
    ms = jnp.mean(x * x, axis=-1, keepdims=True)
    o_ref[...] = (x * lax.rsqrt(ms + RMS_EPS) * g_ref[...]).astype(o_ref.dtype)


def _rmsnorm_bf16(x, g, tm):
    m, d = x.shape
    return pl.pallas_call(
        _rms_kernel, grid=(m // tm,),
        in_specs=[pl.BlockSpec((tm, d), lambda i: (i, 0)), pl.BlockSpec((1, d), lambda i: (0, 0))],
        out_specs=pl.BlockSpec((tm, d), lambda i: (i, 0)),
        out_shape=jax.ShapeDtypeStruct((m, d), BF16), compiler_params=_params(1), name="rmsnorm",
    )(x, g.reshape(1, d))


def _mm_kernel(x_ref, w_ref, o_ref):
    o_ref[...] = jnp.dot(x_ref[...].astype(BF16), w_ref[...], preferred_element_type=F32)


def _in_proj(xn, w, tm, tn):
    m, k = xn.shape
    n = w.shape[1]
    return pl.pallas_call(
        _mm_kernel, grid=(n // tn, m // tm),
        in_specs=[pl.BlockSpec((tm, k), lambda j, i: (i, 0)), pl.BlockSpec((k, tn), lambda j, i: (0, j))],
        out_specs=pl.BlockSpec((tm, tn), lambda j, i: (i, j)),
        out_shape=jax.ShapeDtypeStruct((m, n), F32), compiler_params=_params(2), name="in_proj",
    )(xn, w)


def _ssm_input(proj, b_bd, tm):
    m = proj.shape[0]
    return pl.pallas_call(
        _mm_kernel, grid=(8, m // tm),
        in_specs=[pl.BlockSpec((tm, 256), lambda j, i: (i, j % 4)),
                  pl.BlockSpec((256, 1024), lambda j, i: (j % 4, j))],
        out_specs=pl.BlockSpec((tm, 1024), lambda j, i: (i, j)),
        out_shape=jax.ShapeDtypeStruct((m, 2 * NS), F32), compiler_params=_params(2), name="ssm_input",
    )(proj, b_bd)


def _mm2_kernel(xr_ref, xi_ref, wr_ref, wi_ref, o_ref):
    o_ref[...] = (jnp.dot(xr_ref[...].astype(BF16), wr_ref[...], preferred_element_type=F32)
                  + jnp.dot(xi_ref[...].astype(BF16), wi_ref[...], preferred_element_type=F32))


def _ssm_output(hs, c_re_bd, c_im_bd, tm):
    m = hs.shape[0]
    return pl.pallas_call(
        _mm2_kernel, grid=(4, m // tm),
        in_specs=[pl.BlockSpec((tm, 1024), lambda j, i: (i, j)),
                  pl.BlockSpec((tm, 1024), lambda j, i: (i, 4 + j)),
                  pl.BlockSpec((1024, 256), lambda j, i: (j, j)),
                  pl.BlockSpec((1024, 256), lambda j, i: (j, j))],
        out_specs=pl.BlockSpec((tm, 256), lambda j, i: (i, j)),
        out_shape=jax.ShapeDtypeStruct((m, SSM_WIDTH), F32), compiler_params=_params(2), name="ssm_output",
    )(hs, hs, c_re_bd, c_im_bd)


def _scan_kernel(bu_ref, a_ref, hs_ref, st_ref):
    @pl.when(pl.program_id(0) == 0)
    def _():
        st_ref[...] = jnp.zeros_like(st_ref)

    ar = a_ref[:, :NS]
    ai = a_ref[:, NS:]

    def body(t, carry):
        hr, hi = carry
        b = bu_ref[pl.ds(t, 1), :]
        nr = ar * hr - ai * hi + b[:, :NS]
        ni = ar * hi + ai * hr + b[:, NS:]
        hs_ref[pl.ds(t, 1), :] = jnp.concatenate([nr, ni], axis=1)
        return nr, ni

    hr, hi = lax.fori_loop(0, SCAN_T, body, (st_ref[:, :NS], st_ref[:, NS:]))
    st_ref[...] = jnp.concatenate([hr, hi], axis=1)


def _ssm_scan(bu, a_row):
    m = bu.shape[0]
    return pl.pallas_call(
        _scan_kernel, grid=(m // SCAN_T,),
        in_specs=[pl.BlockSpec((SCAN_T, 2 * NS), lambda i: (i, 0)),
                  pl.BlockSpec((1, 2 * NS), lambda i: (0, 0))],
        out_specs=pl.BlockSpec((SCAN_T, 2 * NS), lambda i: (i, 0)),
        out_shape=jax.ShapeDtypeStruct((m, 2 * NS), F32),
        scratch_shapes=[pltpu.VMEM((1, 2 * NS), F32)],
        compiler_params=_params(1), name="ssm_scan",
    )(bu, a_row)


def _step_kernel(bu_ref, h0_ref, a_ref, hs_ref):
    ar = a_ref[:, :NS]
    ai = a_ref[:, NS:]
    hr = h0_ref[:, :NS]
    hi = h0_ref[:, NS:]
    nr = ar * hr - ai * hi + bu_ref[:, :NS]
    ni = ar * hi + ai * hr + bu_ref[:, NS:]
    hs_ref[...] = jnp.concatenate([nr, ni], axis=1)


def _ssm_step(bu, h0, a_row):
    m = bu.shape[0]
    return pl.pallas_call(
        _step_kernel, out_shape=jax.ShapeDtypeStruct((m, 2 * NS), F32),
        compiler_params=pltpu.CompilerParams(vmem_limit_bytes=VMEM_LIMIT_BYTES), name="ssm_step",
    )(bu, h0, a_row)


def _glu_kernel(y_ref, u_ref, za_ref, d_ref, w_ref, b_ref, o_ref):
    y = y_ref[...] + d_ref[...] * u_ref[...]
    g = y * (lax.erf(y * (1.0 / math.sqrt(2.0))) + 1.0) * 0.5
    t = jnp.dot(g.astype(BF16), w_ref[...], preferred_element_type=F32) + b_ref[...]
    y2 = g * _sigmoid(t)
    za = za_ref[...]
    o_ref[...] = y2 * (za * _sigmoid(za))


def _glu(y, proj, d_row, glu_w, glu_b_row, tm):
    m = y.shape[0]
    w = SSM_WIDTH
    return pl.pallas_call(
        _glu_kernel, grid=(m // tm,),
        in_specs=[pl.BlockSpec((tm, w), lambda i: (i, 0)),
                  pl.BlockSpec((tm, w), lambda i: (i, U_BLK_1024)),
                  pl.BlockSpec((tm, w), lambda i: (i, ZA_BLK_1024)),
                  pl.BlockSpec((1, w), lambda i: (0, 0)),
                  pl.BlockSpec((w, w), lambda i: (0, 0)),
                  pl.BlockSpec((1, w), lambda i: (0, 0))],
        out_specs=pl.BlockSpec((tm, w), lambda i: (i, 0)),
        out_shape=jax.ShapeDtypeStruct((m, w), F32), compiler_params=_params(1), name="glu",
    )(y, proj, proj, d_row, glu_w, glu_b_row)


def _softplus_parts(z):
    return jnp.maximum(z, 0.0) + jnp.log(1.0 + jnp.exp(-jnp.abs(z)))


def _later_sum(lk, tri):
    hi = lk.astype(BF16)
    lo = (lk - hi.astype(F32)).astype(BF16)
    return (jnp.dot(hi, tri, preferred_element_type=F32) + jnp.dot(lo, tri, preferred_element_type=F32))


def _attn_kernel(bias_ref, q_ref, k_ref, v_ref, o_ref):
    h = pl.program_id(0)
    qi = pl.program_id(1)
    bias = bias_ref[h]
    qb = q_ref[...].astype(BF16)
    row = lax.broadcasted_iota(jnp.int32, (TQ, TQ), 0)
    col = lax.broadcasted_iota(jnp.int32, (TQ, TQ), 1)
    tri = jnp.where(row > col, 1.0, 0.0).astype(BF16)

    def body(j, carry):
        acc, c = carry
        kb = qi - j
        ks = pl.multiple_of(kb * TQ, TQ)
        k = k_ref[pl.ds(ks, TQ), :].astype(BF16)
        v = v_ref[pl.ds(ks, TQ), :].astype(BF16)
        z = lax.dot_general(qb, k, (((1,), (1,)), ((), ())), preferred_element_type=F32) * SB_SCALE + bias
        valid = (ks + col) < (qi * TQ + row)
        sp = _softplus_parts(z)
        lk = jnp.where(valid, -sp, 0.0)
        later = _later_sum(lk, tri) + c
        w = jnp.where(valid, jnp.exp(z - sp + later), 0.0)
        acc = acc + jnp.dot(w.astype(BF16), v, preferred_element_type=F32)
        c = c + jnp.sum(lk, axis=1, keepdims=True)
        return acc, c

    acc0 = jnp.zeros((TQ, SB_HEAD_DIM), F32)
    c0 = jnp.zeros((TQ, 1), F32)
    acc, _ = lax.fori_loop(0, qi + 1, body, (acc0, c0))
    o_ref[...] = acc


def _attn_prompt(proj, sb_bias):
    m = proj.shape[0]
    d = SB_HEAD_DIM
    return pl.pallas_call(
        _attn_kernel, grid=(SB_HEADS, m // TQ),
        in_specs=[pl.BlockSpec(memory_space=pltpu.SMEM),
                  pl.BlockSpec((TQ, d), lambda h, i: (i, Q_BLK_128 + h)),
                  pl.BlockSpec((m, d), lambda h, i: (0, K_BLK_128 + h)),
                  pl.BlockSpec((m, d), lambda h, i: (0, V_BLK_128 + h))],
        out_specs=pl.BlockSpec((TQ, d), lambda h, i: (i, h)),
        out_shape=jax.ShapeDtypeStruct((m, SB_WIDTH), F32), compiler_params=_params(2), name="attn_prompt",
    )(sb_bias, proj, proj, proj)


def _dec_kernel(pt_ref, q_ref, bias_ref, k_ref, v_ref, o_ref, acc_ref, c_ref):
    p = pl.program_id(1)

    @pl.when(p == 0)
    def _():
        acc_ref[...] = jnp.zeros_like(acc_ref)
        c_ref[...] = jnp.zeros_like(c_ref)

    q = q_ref[...].astype(BF16)
    head = lax.broadcasted_iota(jnp.int32, (SB_HEADS, PAGE), 0)
    row = lax.broadcasted_iota(jnp.int32, (PAGE, PAGE), 0)
    col = lax.broadcasted_iota(jnp.int32, (PAGE, PAGE), 1)
    tri = jnp.where(row > col, 1.0, 0.0).astype(BF16)

    z = jnp.zeros((SB_HEADS, PAGE), F32)
    for h in range(SB_HEADS):
        kh = k_ref[:, h * SB_HEAD_DIM:(h + 1) * SB_HEAD_DIM].astype(BF16)
        zh = lax.dot_general(q, kh, (((1,), (1,)), ((), ())), preferred_element_type=F32)
        z = jnp.where(head == h, zh, z)
    z = z * SB_SCALE + bias_ref[...]
    sp = _softplus_parts(z)
    lk = -sp
    later = _later_sum(lk, tri) + c_ref[...]
    w = jnp.exp(z - sp + later).astype(BF16)
    acc = acc_ref[...]
    for h in range(SB_HEADS):
        vh = v_ref[:, h * SB_HEAD_DIM:(h + 1) * SB_HEAD_DIM].astype(BF16)
        oh = jnp.dot(w, vh, preferred_element_type=F32)
        acc = acc + jnp.where(head == h, oh, 0.0)
    acc_ref[...] = acc
    c_ref[...] = c_ref[...] + jnp.sum(lk, axis=1, keepdims=True)

    @pl.when(p == pl.num_programs(1) - 1)
    def _():
        o_ref[...] = acc_ref[...]


def _attn_decode(q, bias_b, cache_k, cache_v, page_table):
    b, n_pages = page_table.shape
    n_phys = cache_k.shape[0]
    pt = page_table.reshape(b * n_pages)

    def page_map(i, p, pt_ref):
        return (pt_ref[i * n_pages + (n_pages - 1 - p)], 0, 0)

    grid_spec = pltpu.PrefetchScalarGridSpec(
        num_scalar_prefetch=1, grid=(b, n_pages),
        in_specs=[pl.BlockSpec((None, SB_HEADS, SB_HEAD_DIM), lambda i, p, pt_ref: (i, 0, 0)),
                  pl.BlockSpec((SB_HEADS, PAGE), lambda i, p, pt_ref: (0, 0)),
                  pl.BlockSpec((None, PAGE, SB_WIDTH), page_map),
                  pl.BlockSpec((None, PAGE, SB_WIDTH), page_map)],
        out_specs=pl.BlockSpec((None, SB_HEADS, SB_HEAD_DIM), lambda i, p, pt_ref: (i, 0, 0)),
        scratch_shapes=[pltpu.VMEM((SB_HEADS, SB_HEAD_DIM), F32), pltpu.VMEM((SB_HEADS, PAGE), F32)])
    return pl.pallas_call(
        _dec_kernel, grid_spec=grid_spec,
        out_shape=jax.ShapeDtypeStruct((b, SB_HEADS, SB_HEAD_DIM), F32),
        compiler_params=_params(2), name="attn_decode",
    )(pt, q, bias_b, cache_k.reshape(n_phys, PAGE, SB_WIDTH), cache_v.reshape(n_phys, PAGE, SB_WIDTH))


def _tail_kernel(ya_ref, o_ref, zb_ref, ga_ref, gb_ref, x_ref, wa_ref, wb_ref, wo_ref, nf_ref, out_ref):
    zb = zb_ref[...]
    ob = o_ref[...] * (zb * _sigmoid(zb))
    pa = jnp.dot(ya_ref[...].astype(BF16), wa_ref[...], preferred_element_type=F32)
    pb = jnp.dot(ob.astype(BF16), wb_ref[...], preferred_element_type=F32)
    merged = _sigmoid(ga_ref[...]) * pa + _sigmoid(gb_ref[...]) * pb
    hid = x_ref[...] + jnp.dot(merged.astype(BF16), wo_ref[...], preferred_element_type=F32)
    ms = jnp.mean(hid * hid, axis=-1, keepdims=True)
    out_ref[...] = hid * lax.rsqrt(ms + RMS_EPS) * nf_ref[...]


def _tail(ya, o, proj, x, w_a, w_b, w_o, nf_row, tm):
    m = x.shape[0]
    d = D_MODEL
    w = SSM_WIDTH
    return pl.pallas_call(
        _tail_kernel, grid=(m // tm,),
        in_specs=[pl.BlockSpec((tm, w), lambda i: (i, 0)),
                  pl.BlockSpec((tm, w), lambda i: (i, 0)),
                  pl.BlockSpec((tm, w), lambda i: (i, ZB_BLK_1024)),
                  pl.BlockSpec((tm, d), lambda i: (i, GA_BLK_2048)),
                  pl.BlockSpec((tm, d), lambda i: (i, GB_BLK_2048)),
                  pl.BlockSpec((tm, d), lambda i: (i, 0)),
                  pl.BlockSpec((w, d), lambda i: (0, 0)),
                  pl.BlockSpec((w, d), lambda i: (0, 0)),
                  pl.BlockSpec((d, d), lambda i: (0, 0)),
                  pl.BlockSpec((1, d), lambda i: (0, 0))],
        out_specs=pl.BlockSpec((tm, d), lambda i: (i, 0)),
        out_shape=jax.ShapeDtypeStruct((m, d), F32), compiler_params=_params(1), name="tail",
    )(ya, o, proj, proj, proj, x, w_a, w_b, w_o, nf_row)


def _block_diag(blocks):
    g, r, c = blocks.shape
    eye = jnp.eye(g, dtype=blocks.dtype)
    return (blocks[:, :, None, :] * eye[:, None, :, None]).reshape(g * r, g * c)


def _ssm_params(a_re, a_im, log_dt, b_re, b_im, c_re, c_im):
    lam = lax.complex(a_re.astype(F32), a_im.astype(F32))
    dt = jnp.exp(log_dt.astype(F32))[:, None]
    a_bar = jnp.exp(lam * dt)
    b_bar = ((a_bar - 1.0) / lam)[:, :, None] * lax.complex(b_re.astype(F32), b_im.astype(F32))
    a_row = jnp.concatenate([jnp.real(a_bar).reshape(1, NS), jnp.imag(a_bar).reshape(1, NS)], axis=1)
    b_bd = jnp.concatenate([_block_diag(jnp.real(b_bar).transpose(0, 2, 1)),
                            _block_diag(jnp.imag(b_bar).transpose(0, 2, 1))], axis=1).astype(BF16)
    c_re_bd = _block_diag(c_re.astype(F32).transpose(0, 2, 1)).astype(BF16)
    c_im_bd = _block_diag(-c_im.astype(F32).transpose(0, 2, 1)).astype(BF16)
    return a_row, b_bd, c_re_bd, c_im_bd


def kernel(x_prompt, x_sample, cache_k, cache_v, state_ssm_re, state_ssm_im, page_table, norm_g, w_in, ssm_A_re, ssm_A_im, ssm_log_dt, ssm_B_re, ssm_B_im, ssm_C_re, ssm_C_im, ssm_D, glu_w, glu_b, sb_bias, w_branch_a, w_branch_b, w_out, norm_f):
    seq = x_prompt.shape[1]
    dec_b = x_sample.shape[0]
    xp = x_prompt.reshape(seq, D_MODEL)
    xs = x_sample.reshape(dec_b, D_MODEL)

    w_in_b = w_in[0].astype(BF16)
    glu_w_b = glu_w[0].astype(BF16)
    w_a_b = w_branch_a[0].astype(BF16)
    w_b_b = w_branch_b[0].astype(BF16)
    w_o_b = w_out[0].astype(BF16)
    a_row, b_bd, c_re_bd, c_im_bd = _ssm_params(ssm_A_re[0], ssm_A_im[0], ssm_log_dt[0], ssm_B_re[0],
                                                 ssm_B_im[0], ssm_C_re[0], ssm_C_im[0])
    d_row = ssm_D[0].astype(F32).reshape(1, SSM_WIDTH)
    glu_b_row = glu_b[0].astype(F32).reshape(1, SSM_WIDTH)
    nf_row = norm_f.astype(F32).reshape(1, D_MODEL)
    bias = sb_bias[0].astype(F32)

    proj_p = _in_proj(_rmsnorm_bf16(xp, norm_g[0], 512), w_in_b, 512, 1024)
    hs_p = _ssm_scan(_ssm_input(proj_p, b_bd, 512), a_row)
    y_p = _ssm_output(hs_p, c_re_bd, c_im_bd, 512)
    ya_p = _glu(y_p, proj_p, d_row, glu_w_b, glu_b_row, 256)
    o_p = _attn_prompt(proj_p, bias)
    out_p = _tail(ya_p, o_p, proj_p, xp, w_a_b, w_b_b, w_o_b, nf_row, 128)

    proj_s = _in_proj(_rmsnorm_bf16(xs, norm_g[0], dec_b), w_in_b, dec_b, 1024)
    h0 = jnp.concatenate([state_ssm_re[0].astype(F32).reshape(dec_b, NS),
                          state_ssm_im[0].astype(F32).reshape(dec_b, NS)], axis=1)
    hs_s = _ssm_step(_ssm_input(proj_s, b_bd, dec_b), h0, a_row)
    y_s = _ssm_output(hs_s, c_re_bd, c_im_bd, dec_b)
    ya_s = _glu(y_s, proj_s, d_row, glu_w_b, glu_b_row, dec_b)
    q_s = proj_s[:, 2 * SSM_WIDTH:2 * SSM_WIDTH + SB_WIDTH].reshape(dec_b, SB_HEADS, SB_HEAD_DIM)
    bias_b = jnp.broadcast_to(bias[:, None], (SB_HEADS, PAGE))
    o_s = _attn_decode(q_s, bias_b, cache_k[0], cache_v[0], page_table).reshape(dec_b, SB_WIDTH)
    out_s = _tail(ya_s, o_s, proj_s, xs, w_a_b, w_b_b, w_o_b, nf_row, dec_b)

    k0 = 2 * SSM_WIDTH + SB_WIDTH
    v0 = k0 + SB_WIDTH
    kv_shape_p = (1, 1, seq, SB_HEADS, SB_HEAD_DIM)
    kv_shape_s = (1, dec_b, 1, SB_HEADS, SB_HEAD_DIM)
    st_shape_p = (1, 1, SSM_GROUPS, SSM_STATE)
    st_shape_s = (1, dec_b, SSM_GROUPS, SSM_STATE)
    return (out_p.reshape(1, seq, D_MODEL),
            out_s.reshape(dec_b, 1, D_MODEL),
            proj_p[:, k0:k0 + SB_WIDTH].reshape(kv_shape_p),
            proj_p[:, v0:v0 + SB_WIDTH].reshape(kv_shape_p),
            proj_s[:, k0:k0 + SB_WIDTH].reshape(kv_shape_s),
            proj_s[:, v0:v0 + SB_WIDTH].reshape(kv_shape_s),
            hs_p[seq - 1, :NS].reshape(st_shape_p),
            hs_p[seq - 1, NS:].reshape(st_shape_p),
            hs_s[:, :NS].reshape(st_shape_s),
            hs_s[:, NS:].reshape(st_shape_s))
```
